```python
import math
import jax, jax.numpy as jnp
from jax import lax
import numpy as np

D_MODEL = 2048
BATCH = 2
SEQ = 8192
DEPTH = 1
DEC_BATCH = 32
DEC_SEQ = 1
PAST_LEN = 16384
PAGE_SIZE = 128

CONV_CH = D_MODEL // 2
CONV_WIDTH = 31
HEAD_DIM = 128
N_HEADS = (D_MODEL // 2) // HEAD_DIM
ATTN_W = N_HEADS * HEAD_DIM
MOBA_BLOCK = 256
MOBA_TOP_K = 3
Q_BLOCK = 64
D_FF = ((8 * D_MODEL // 3 + 127) // 128) * 128
FFN_CONV_WIDTH = 3
IN_COLS = 2 * CONV_CH + 3 * ATTN_W + 2 * D_MODEL
SPLITS = [CONV_CH, 2 * CONV_CH, 2 * CONV_CH + ATTN_W, 2 * CONV_CH + 2 * ATTN_W,
          2 * CONV_CH + 3 * ATTN_W, 2 * CONV_CH + 3 * ATTN_W + D_MODEL]
EPS = 1e-6

kernel_name = "hybrid_conformer_moba_convffn_step"


def rms_norm(x, g):
    xf = x.astype(jnp.float32)
    y = xf * lax.rsqrt(jnp.mean(xf * xf, axis=-1, keepdims=True) + EPS)
    return (y * g.astype(jnp.float32)).astype(x.dtype)


def layer_norm(x, g, b):
    xf = x.astype(jnp.float32)
    mu = jnp.mean(xf, axis=-1, keepdims=True)
    xc = xf - mu
    y = xc * lax.rsqrt(jnp.mean(xc * xc, axis=-1, keepdims=True) + EPS)
    return (y * g.astype(jnp.float32) + b.astype(jnp.float32)).astype(x.dtype)


def causal_dwconv(x_ext, w, b):
    c = x_ext.shape[-1]
    y = lax.conv_general_dilated(x_ext, w[:, None, :].astype(x_ext.dtype), window_strides=(1,),
                                 padding='VALID', dimension_numbers=('NWC', 'WIO', 'NWC'),
                                 feature_group_count=c)
    return y + b.astype(y.dtype)


def moba_attention(q, k_all, v_all, seq_len):
    B, T, H, Dh = q.shape
    nb = k_all.shape[1] // MOBA_BLOCK
    kb = k_all.reshape(B, nb, MOBA_BLOCK, H, Dh)
    vb = v_all.reshape(B, nb, MOBA_BLOCK, H, Dh)
    kmean = jnp.mean(kb, axis=2, dtype=jnp.float32)
    q_pos = (seq_len - T) + jnp.arange(T)
    q_blk = q_pos // MOBA_BLOCK
    gate = jnp.einsum('bthd,bnhd->bhtn', q.astype(jnp.float32), kmean)
    fully_past = jnp.arange(nb)[None, :] < q_blk[:, None]
    gate = jnp.where(fully_past, gate, -jnp.inf)
    n_sel = min(MOBA_TOP_K, nb)
    _, sel = lax.top_k(gate, n_sel)
    sel_ok = sel < q_blk[:, None]
    own = jnp.broadcast_to(q_blk[:, None], (B, H, T, 1))
    blk = jnp.concatenate([sel, own], axis=-1).astype(jnp.int32)
    ok = jnp.concatenate([sel_ok, jnp.ones((B, H, T, 1), bool)], axis=-1)
    qb = max(d for d in range(1, min(Q_BLOCK, T) + 1) if T % d == 0)
    nqb = T // qb

    def to_blocks(a):
        return a.transpose(0, 2, 1, 3).reshape(B * nqb, qb, H, a.shape[-1])

    q_c = q.reshape(B * nqb, qb, H, Dh)
    pos_c = jnp.tile(q_pos.reshape(nqb, qb), (B, 1))
    b_c = jnp.repeat(jnp.arange(B), nqb)
    head = jnp.arange(H)[None, :, None]
    offs = jnp.arange(MOBA_BLOCK)
    scale = HEAD_DIM ** -0.5

    def attend(args):
        qc, bc, okc, pc, bi = args
        k_sel = kb[bi][bc, :, head]
        v_sel = vb[bi][bc, :, head]
        s = jnp.einsum('qhd,qhkjd->qhkj', qc.astype(jnp.float32), k_sel.astype(jnp.float32)) * scale
        kpos = bc[..., None] * MOBA_BLOCK + offs
        mask = okc[..., None] & (kpos <= pc[:, None, None, None])
        s = jnp.where(mask, s, -jnp.inf)
        p = jax.nn.softmax(s.reshape(qb, H, -1), axis=-1).reshape(s.shape)
        o = jnp.einsum('qhkj,qhkjd->qhd', p, v_sel.astype(jnp.float32))
        return o.astype(qc.dtype)

    out = lax.map(attend, (q_c, to_blocks(blk), to_blocks(ok), pos_c, b_c))
    return out.reshape(B, T, H, Dh)


def trunk_layer(x, conv_prefix, ffn_prefix, cache_k, cache_v, page_table, layer,
                norm1_g, w_in, b_gate, q_norm_g, k_norm_g, w_dw, b_dw, ln_g, ln_b,
                w_conv_out, w_attn_out, w_out, norm2_g, w_ffn_up, w_ffn_dw, b_ffn_dw, w_ffn_down):
    B, T, _ = x.shape
    xn = rms_norm(x, norm1_g)
    z = xn @ w_in
    a_val, a_gate, q, k, v, g_a, g_b = jnp.split(z, SPLITS, axis=-1)
    u = a_val * jax.nn.sigmoid(a_gate)
    u_ext = jnp.concatenate([conv_prefix.astype(u.dtype), u], axis=1)
    a = causal_dwconv(u_ext, w_dw, b_dw)
    a = jax.nn.silu(layer_norm(a, ln_g, ln_b)) @ w_conv_out
    conv_state = u_ext[:, -(CONV_WIDTH - 1):]
    q = rms_norm(q.reshape(B, T, N_HEADS, HEAD_DIM), q_norm_g)
    k = rms_norm(k.reshape(B, T, N_HEADS, HEAD_DIM), k_norm_g)
    v = v.reshape(B, T, N_HEADS, HEAD_DIM)
    past_len = 0 if cache_k is None else page_table.shape[1] * cache_k.shape[2]
    seq_len = past_len + T
    pad = (-seq_len) % MOBA_BLOCK
    zpad = jnp.zeros((B, pad, N_HEADS, HEAD_DIM), k.dtype)
    if cache_k is None:
        k_all = jnp.concatenate([k, zpad], axis=1)
        v_all = jnp.concatenate([v, zpad], axis=1)
    else:
        k_all = jnp.concatenate([cache_k[layer, page_table].reshape(B, past_len, N_HEADS, HEAD_DIM).astype(k.dtype), k, zpad], axis=1)
        v_all = jnp.concatenate([cache_v[layer, page_table].reshape(B, past_len, N_HEADS, HEAD_DIM).astype(v.dtype), v, zpad], axis=1)
    attn = moba_attention(q, k_all, v_all, seq_len)
    b_out = attn.reshape(B, T, ATTN_W) @ w_attn_out
    m = jax.nn.sigmoid(g_a + b_gate[:D_MODEL]) * a + jax.nn.sigmoid(g_b + b_gate[D_MODEL:]) * b_out
    h = x + m @ w_out
    up = rms_norm(h, norm2_g) @ w_ffn_up
    up_ext = jnp.concatenate([ffn_prefix.astype(up.dtype), up], axis=1)
    c = causal_dwconv(up_ext, w_ffn_dw, b_ffn_dw)
    c_gate, c_val = jnp.split(c, 2, axis=-1)
    y = h + (jax.nn.silu(c_gate) * c_val) @ w_ffn_down
    ffn_state = up_ext[:, -(FFN_CONV_WIDTH - 1):]
    return y, k, v, conv_state, ffn_state


def setup_inputs(seed: int = 0) -> dict:
    key = jax.random.key(seed)
    ks = jax.random.split(key, 24)

    def nrm(k, shape, scale):
        return jax.random.normal(k, shape, jnp.float32) * scale

    n_pages = PAST_LEN // PAGE_SIZE
    n_used = DEC_BATCH * n_pages
    n_pool = n_used + (n_used + 3) // 4
    page_table = jax.random.permutation(ks[4], n_pool)[:n_used].reshape(DEC_BATCH, n_pages).astype(jnp.int32)
    return {
        "x_prompt": nrm(ks[0], (BATCH, SEQ, D_MODEL), 1.0),
        "x_sample": nrm(ks[1], (DEC_BATCH, DEC_SEQ, D_MODEL), 1.0),
        "cache_k": nrm(ks[2], (DEPTH, n_pool, PAGE_SIZE, N_HEADS, HEAD_DIM), 1.0),
        "cache_v": nrm(ks[3], (DEPTH, n_pool, PAGE_SIZE, N_HEADS, HEAD_DIM), 1.0),
        "page_table": page_table,
        "state_conv": nrm(ks[5], (DEPTH, DEC_BATCH, CONV_WIDTH - 1, CONV_CH), 0.5),
        "state_ffn_conv": nrm(ks[6], (DEPTH, DEC_BATCH, FFN_CONV_WIDTH - 1, 2 * D_FF), 1.0),
        "norm1_g": 1.0 + nrm(ks[7], (DEPTH, D_MODEL), 0.02),
        "w_in": nrm(ks[8], (DEPTH, D_MODEL, IN_COLS), D_MODEL ** -0.5),
        "b_gate": nrm(ks[9], (DEPTH, 2 * D_MODEL), 0.02),
        "q_norm_g": 1.0 + nrm(ks[10], (DEPTH, HEAD_DIM), 0.02),
        "k_norm_g": 1.0 + nrm(ks[11], (DEPTH, HEAD_DIM), 0.02),
        "w_dw": nrm(ks[12], (DEPTH, CONV_WIDTH, CONV_CH), CONV_WIDTH ** -0.5),
        "b_dw": nrm(ks[13], (DEPTH, CONV_CH), 0.02),
        "ln_g": 1.0 + nrm(ks[14], (DEPTH, CONV_CH), 0.02),
        "ln_b": nrm(ks[15], (DEPTH, CONV_CH), 0.02),
        "w_conv_out": nrm(ks[16], (DEPTH, CONV_CH, D_MODEL), CONV_CH ** -0.5),
        "w_attn_out": nrm(ks[17], (DEPTH, ATTN_W, D_MODEL), ATTN_W ** -0.5),
        "w_out": nrm(ks[18], (DEPTH, D_MODEL, D_MODEL), D_MODEL ** -0.5),
        "norm2_g": 1.0 + nrm(ks[19], (DEPTH, D_MODEL), 0.02),
        "w_ffn_up": nrm(ks[20], (DEPTH, D_MODEL, 2 * D_FF), D_MODEL ** -0.5),
        "w_ffn_dw": nrm(ks[21], (DEPTH, FFN_CONV_WIDTH, 2 * D_FF), FFN_CONV_WIDTH ** -0.5),
        "b_ffn_dw": nrm(ks[22], (DEPTH, 2 * D_FF), 0.02),
        "w_ffn_down": nrm(ks[23], (DEPTH, D_FF, D_MODEL), D_FF ** -0.5),
    }


def reference(x_prompt, x_sample, cache_k, cache_v, page_table, state_conv, state_ffn_conv,
              norm1_g, w_in, b_gate, q_norm_g, k_norm_g, w_dw, b_dw, ln_g, ln_b,
              w_conv_out, w_attn_out, w_out, norm2_g, w_ffn_up, w_ffn_dw, b_ffn_dw, w_ffn_down):
    y_prompt, y_sample = x_prompt, x_sample
    kp, vp, ks_, vs_, cp, cs, fp, fs = [], [], [], [], [], [], [], []
    for l in range(DEPTH):
        w = (norm1_g[l], w_in[l], b_gate[l], q_norm_g[l], k_norm_g[l], w_dw[l], b_dw[l], ln_g[l], ln_b[l],
             w_conv_out[l], w_attn_out[l], w_out[l], norm2_g[l], w_ffn_up[l], w_ffn_dw[l], b_ffn_dw[l], w_ffn_down[l])
        conv0 = jnp.zeros((x_prompt.shape[0], CONV_WIDTH - 1, CONV_CH), x_prompt.dtype)
        ffn0 = jnp.zeros((x_prompt.shape[0], FFN_CONV_WIDTH - 1, 2 * D_FF), x_prompt.dtype)
        y_prompt, k1, v1, c1, f1 = trunk_layer(y_prompt, conv0, ffn0, None, None, None, l, *w)
        y_sample, k2, v2, c2, f2 = trunk_layer(y_sample, state_conv[l], state_ffn_conv[l],
                                               cache_k, cache_v, page_table, l, *w)
        kp.append(k1); vp.append(v1); cp.append(c1); fp.append(f1)
        ks_.append(k2); vs_.append(v2); cs.append(c2); fs.append(f2)
    k_prompt = jnp.stack(kp)
    v_prompt = jnp.stack(vp)
    k_sample = jnp.stack(ks_)
    v_sample = jnp.stack(vs_)
    conv_state_prompt = jnp.stack(cp)
    conv_state_sample = jnp.stack(cs)
    ffn_state_prompt = jnp.stack(fp)
    ffn_state_sample = jnp.stack(fs)
    return (y_prompt, y_sample, k_prompt, v_prompt, k_sample, v_sample,
            conv_state_prompt, conv_state_sample, ffn_state_prompt, ffn_state_sample)
```

```python
import functools

import jax
import jax.numpy as jnp
from jax import lax
from jax.experimental import pallas as pl
from jax.experimental.pallas import tpu as pltpu

F32 = jnp.float32
BF16 = jnp.bfloat16

D_MODEL = 2048
CONV_CH = 1024
CONV_WIDTH = 31
HEAD_DIM = 128
N_HEADS = 8
ATTN_W = N_HEADS * HEAD_DIM
MOBA_BLOCK = 256
MOBA_TOP_K = 3
D_FF = 5504
FFN_CONV_WIDTH = 3
EPS = 1e-6
PAGE_SIZE = 128

LANES = 128
SUBLANES = 8
D_FF_PAD = 5632
FFN_TILE = 512
N_FFN_TILES = D_FF_PAD // FFN_TILE
VMEM_LIMIT = 56 * 1024 * 1024
CONV_HALO = 32
NT_DIMS = (((1,), (1,)), ((), ()))
NEG_INF = float("-inf")


def _params():
    return pltpu.CompilerParams(vmem_limit_bytes=VMEM_LIMIT)


def _dot(a, b):
    return jnp.dot(a, b, preferred_element_type=F32)


def _dot_nt(a, b):
    return lax.dot_general(a, b, NT_DIMS, preferred_element_type=F32)


def _sigmoid(x):
    return 1.0 / (1.0 + jnp.exp(-x))


def _rmsnorm_kernel(x_ref, g_ref, o_ref):
    x = x_ref[...]
    ms = jnp.mean(x * x, axis=-1, keepdims=True)
    o_ref[...] = (x * lax.rsqrt(ms + EPS) * g_ref[...]).astype(o_ref.dtype)


def _rmsnorm(x, g, tm):
    m, d = x.shape
    return pl.pallas_call(
        _rmsnorm_kernel,
        grid=(m // tm,),
        in_specs=[pl.BlockSpec((tm, d), lambda i: (i, 0)),
                  pl.BlockSpec((1, d), lambda i: (0, 0))],
        out_specs=pl.BlockSpec((tm, d), lambda i: (i, 0)),
        out_shape=jax.ShapeDtypeStruct((m, d), BF16),
        compiler_params=_params(),
        name="rmsnorm",
    )(x, g.reshape(1, d))


def _glu_kernel(x_ref, wv_ref, wg_ref, u_ref):
    x = x_ref[...]
    u_ref[...] = _dot(x, wv_ref[...]) * _sigmoid(_dot(x, wg_ref[...]))


def _glu_proj(xn, w_in, tm):
    m, d = xn.shape
    tn = 512
    nj = CONV_CH // tn
    return pl.pallas_call(
        _glu_kernel,
        grid=(nj, m // tm),
        in_specs=[pl.BlockSpec((tm, d), lambda j, i: (i, 0)),
                  pl.BlockSpec((d, tn), lambda j, i: (0, j)),
                  pl.BlockSpec((d, tn), lambda j, i: (0, nj + j))],
        out_specs=pl.BlockSpec((tm, tn), lambda j, i: (i, j)),
        out_shape=jax.ShapeDtypeStruct((m, CONV_CH), F32),
        compiler_params=_params(),
        name="glu_proj",
    )(xn, w_in, w_in)


def _head_rmsnorm(z, g):
    outs = []
    for h in range(N_HEADS):
        zh = z[:, h * HEAD_DIM:(h + 1) * HEAD_DIM]
        ms = jnp.mean(zh * zh, axis=-1, keepdims=True)
        outs.append(zh * lax.rsqrt(ms + EPS) * g)
    return jnp.concatenate(outs, axis=-1)


def _q_kernel(x_ref, w_ref, g_ref, q_ref):
    q_ref[...] = _head_rmsnorm(_dot(x_ref[...], w_ref[...]), g_ref[...]).astype(q_ref.dtype)


def _k_kernel(x_ref, w_ref, g_ref, k_ref, kb_ref, *km_refs):
    k = _head_rmsnorm(_dot(x_ref[...], w_ref[...]), g_ref[...])
    k_ref[...] = k
    kb_ref[...] = k.astype(kb_ref.dtype)
    if km_refs:
        km_ref, = km_refs
        for r in range(k.shape[0] // MOBA_BLOCK):
            blk = k[r * MOBA_BLOCK:(r + 1) * MOBA_BLOCK, :]
            km_ref[0, r:r + 1, :] = jnp.mean(blk, axis=0, keepdims=True)


def _v_kernel(x_ref, w_ref, v_ref, vb_ref):
    v = _dot(x_ref[...], w_ref[...])
    v_ref[...] = v
    vb_ref[...] = v.astype(vb_ref.dtype)


def _head_proj(kernel, xn, w_in, col_block, tm, extra_in, out_shapes, out_specs, name):
    m, d = xn.shape
    in_specs = [pl.BlockSpec((tm, d), lambda i: (i, 0)),
                pl.BlockSpec((d, ATTN_W), lambda i: (0, col_block))]
    in_specs += [pl.BlockSpec(e.shape, lambda i: (0, 0)) for e in extra_in]
    return pl.pallas_call(
        kernel,
        grid=(m // tm,),
        in_specs=in_specs,
        out_specs=out_specs,
        out_shape=out_shapes,
        compiler_params=_params(),
        name=name,
    )(xn, w_in, *extra_in)


def _qkv_proj(xn, w_in, q_g, k_g, tm, q_dtype, with_kmean):
    m = xn.shape[0]
    row_spec = pl.BlockSpec((tm, ATTN_W), lambda i: (i, 0))
    f32_rows = jax.ShapeDtypeStruct((m, ATTN_W), F32)
    bf_rows = jax.ShapeDtypeStruct((m, ATTN_W), BF16)
    first_block = 2 * CONV_CH // ATTN_W
    q = _head_proj(_q_kernel, xn, w_in, first_block, tm, [q_g.reshape(1, HEAD_DIM)],
                   jax.ShapeDtypeStruct((m, ATTN_W), q_dtype), row_spec, "q_proj")
    k_shapes, k_specs = [f32_rows, bf_rows], [row_spec, row_spec]
    if with_kmean:
        nb = tm // MOBA_BLOCK
        k_shapes.append(jax.ShapeDtypeStruct((m // tm, nb, ATTN_W), F32))
        k_specs.append(pl.BlockSpec((1, nb, ATTN_W), lambda i: (i, 0, 0)))
    k_out = _head_proj(_k_kernel, xn, w_in, first_block + 1, tm, [k_g.reshape(1, HEAD_DIM)],
                       k_shapes, k_specs, "k_proj")
    v, v_bf = _head_proj(_v_kernel, xn, w_in, first_block + 2, tm, [],
                         [f32_rows, bf_rows], [row_spec, row_spec], "v_proj")
    return q, k_out, v, v_bf


def _gate_kernel(x_ref, w_ref, b_ref, o_ref):
    o_ref[...] = _sigmoid(_dot(x_ref[...], w_ref[...]) + b_ref[...])


def _gate_proj(xn, w_in, b_gate, tm):
    m, d = xn.shape
    tn = 1024
    first_block = (2 * CONV_CH + 3 * ATTN_W) // tn
    return pl.pallas_call(
        _gate_kernel,
        grid=(2 * D_MODEL // tn, m // tm),
        in_specs=[pl.BlockSpec((tm, d), lambda j, i: (i, 0)),
                  pl.BlockSpec((d, tn), lambda j, i: (0, first_block + j)),
                  pl.BlockSpec((1, tn), lambda j, i: (0, j))],
        out_specs=pl.BlockSpec((tm, tn), lambda j, i: (i, j)),
        out_shape=jax.ShapeDtypeStruct((m, 2 * D_MODEL), F32),
        compiler_params=_params(),
        name="gate_proj",
    )(xn, w_in, b_gate.reshape(1, 2 * D_MODEL))


def _ln_silu(y, g, b):
    mu = jnp.mean(y, axis=-1, keepdims=True)
    yc = y - mu
    var = jnp.mean(yc * yc, axis=-1, keepdims=True)
    z = yc * lax.rsqrt(var + EPS) * g + b
    return z * _sigmoid(z)


CONV_ROWS = 32


def _conv_act_kernel(cur_ref, halo_ref, w_ref, b_ref, g_ref, beta_ref, o_ref, ext_ref, wb_ref, y_ref):
    i = pl.program_id(1)
    tc = cur_ref.shape[1]

    @pl.when((pl.program_id(0) == 0) & (i == 0))
    def _():
        for w in range(CONV_WIDTH):
            wb_ref[w * SUBLANES:(w + 1) * SUBLANES, :] = jnp.broadcast_to(
                w_ref[w:w + 1, :], (SUBLANES, CONV_CH))

    halo = jnp.where(i > 0, halo_ref[0], 0.0)
    for c in range(CONV_CH // LANES):
        cs = slice(c * LANES, (c + 1) * LANES)
        ext_ref[c, 0:CONV_HALO, :] = halo[:, cs]
        ext_ref[c, CONV_HALO:, :] = cur_ref[0, :, cs]
    first_tap = CONV_HALO - (CONV_WIDTH - 1)

    def rows(r, carry):
        r0 = pl.multiple_of(r * CONV_ROWS, CONV_ROWS)
        for c in range(CONV_CH // LANES):
            cs = slice(c * LANES, (c + 1) * LANES)
            acc = jnp.broadcast_to(b_ref[:, cs], (CONV_ROWS, LANES))
            for w in range(CONV_WIDTH):
                xw = ext_ref[c, pl.ds(r0 + first_tap + w, CONV_ROWS), :]
                ww = wb_ref[w * SUBLANES:(w + 1) * SUBLANES, cs]
                acc = acc + (xw.reshape(CONV_ROWS // SUBLANES, SUBLANES, LANES) * ww[None]
                             ).reshape(CONV_ROWS, LANES)
            y_ref[pl.ds(r0, CONV_ROWS), cs] = acc
        return carry

    lax.fori_loop(0, tc // CONV_ROWS, rows, 0)
    o_ref[0] = _ln_silu(y_ref[...], g_ref[...], beta_ref[...]).astype(o_ref.dtype)


def _conv_act(u, w_dw, b_dw, ln_g, ln_b, tc):
    b, t, c = u.shape
    per = tc // CONV_HALO
    row = lambda a: a.reshape(1, c)
    return pl.pallas_call(
        _conv_act_kernel,
        grid=(b, t // tc),
        in_specs=[pl.BlockSpec((1, tc, c), lambda bi, i: (bi, i, 0)),
                  pl.BlockSpec((1, CONV_HALO, c), lambda bi, i: (bi, jnp.maximum(i * per - 1, 0), 0)),
                  pl.BlockSpec((CONV_WIDTH, c), lambda bi, i: (0, 0)),
                  pl.BlockSpec((1, c), lambda bi, i: (0, 0)),
                  pl.BlockSpec((1, c), lambda bi, i: (0, 0)),
                  pl.BlockSpec((1, c), lambda bi, i: (0, 0))],
        out_specs=pl.BlockSpec((1, tc, c), lambda bi, i: (bi, i, 0)),
        out_shape=jax.ShapeDtypeStruct((b, t, c), BF16),
        scratch_shapes=[pltpu.VMEM((c // LANES, tc + CONV_HALO, LANES), F32),
                        pltpu.VMEM((CONV_WIDTH * SUBLANES, c), F32),
                        pltpu.VMEM((tc, c), F32)],
        compiler_params=_params(),
        name="conv_act",
    )(u, u, w_dw, row(b_dw), row(ln_g), row(ln_b))


def _conv_step_kernel(state_ref, u_ref, w_ref, b_ref, g_ref, beta_ref, o_ref):
    past = jnp.sum(state_ref[...] * w_ref[0:CONV_WIDTH - 1, :][None], axis=1)
    y = past + u_ref[...] * w_ref[CONV_WIDTH - 1:CONV_WIDTH, :] + b_ref[...]
    o_ref[...] = _ln_silu(y, g_ref[...], beta_ref[...]).astype(o_ref.dtype)


def _conv_step(state, u, w_dw, b_dw, ln_g, ln_b):
    n, c = u.shape
    row = lambda a: a.reshape(1, c)
    return pl.pallas_call(
        _conv_step_kernel,
        out_shape=jax.ShapeDtypeStruct((n, c), BF16),
        compiler_params=_params(),
        name="conv_step",
    )(state, u, w_dw, row(b_dw), row(ln_g), row(ln_b))


def _moba_prompt_kernel(q_ref, k_ref, v_ref, km_ref, o_ref):
    i = pl.program_id(2)
    tq = q_ref.shape[0]
    nb = km_ref.shape[0]
    q = q_ref[...]
    km = km_ref[...]
    km_hi = km.astype(BF16)
    km_lo = (km - km_hi.astype(F32)).astype(BF16)
    gate = _dot_nt(q, km_hi) + _dot_nt(q, km_lo)
    blk_id = lax.broadcasted_iota(jnp.int32, (tq, nb), 1)
    blk_f = blk_id.astype(F32)
    gate = jnp.where(blk_id < i, gate, NEG_INF)
    sel = jnp.zeros((tq, nb), F32)
    for _ in range(MOBA_TOP_K):
        best = jnp.max(gate, axis=1, keepdims=True)
        first = jnp.min(jnp.where(gate == best, blk_f, float(nb)), axis=1, keepdims=True)
        hit = blk_f == first
        sel = jnp.where(hit, jnp.where(best > NEG_INF, 1.0, sel), sel)
        gate = jnp.where(hit, NEG_INF, gate)

    scale = HEAD_DIM ** -0.5
    d0 = pl.multiple_of(i * MOBA_BLOCK, MOBA_BLOCK)
    s = _dot_nt(q, k_ref[pl.ds(d0, MOBA_BLOCK), :]) * scale
    row = lax.broadcasted_iota(jnp.int32, s.shape, 0)
    col = lax.broadcasted_iota(jnp.int32, s.shape, 1)
    s = jnp.where(col <= row, s, NEG_INF)
    m0 = jnp.max(s, axis=1, keepdims=True)
    p = jnp.exp(s - m0)
    l0 = jnp.sum(p, axis=1, keepdims=True)
    acc0 = _dot(p.astype(BF16), v_ref[pl.ds(d0, MOBA_BLOCK), :])

    def past_block(n, carry):
        m, l, acc = carry
        n0 = pl.multiple_of(n * MOBA_BLOCK, MOBA_BLOCK)
        s = _dot_nt(q, k_ref[pl.ds(n0, MOBA_BLOCK), :]) * scale
        picked = jnp.sum(jnp.where(blk_id == n, sel, 0.0), axis=1, keepdims=True)
        s = jnp.where(picked > 0.0, s, NEG_INF)
        m_new = jnp.maximum(m, jnp.max(s, axis=1, keepdims=True))
        alpha = jnp.exp(m - m_new)
        p = jnp.exp(s - m_new)
        l = alpha * l + jnp.sum(p, axis=1, keepdims=True)
        acc = alpha * acc + _dot(p.astype(BF16), v_ref[pl.ds(n0, MOBA_BLOCK), :])
        return m_new, l, acc

    _, l, acc = lax.fori_loop(0, i, past_block, (m0, l0, acc0))
    o_ref[...] = (acc / l).astype(o_ref.dtype)


def _moba_prompt(q_bf, k_bf, v_bf, kmean, batch, seq):
    nb = seq // MOBA_BLOCK
    return pl.pallas_call(
        _moba_prompt_kernel,
        grid=(batch, N_HEADS, nb),
        in_specs=[pl.BlockSpec((MOBA_BLOCK, HEAD_DIM), lambda b, h, i: (b * nb + i, h)),
                  pl.BlockSpec((seq, HEAD_DIM), lambda b, h, i: (b, h)),
                  pl.BlockSpec((seq, HEAD_DIM), lambda b, h, i: (b, h)),
                  pl.BlockSpec((nb, HEAD_DIM), lambda b, h, i: (b, h))],
        out_specs=pl.BlockSpec((MOBA_BLOCK, HEAD_DIM), lambda b, h, i: (b * nb + i, h)),
        out_shape=jax.ShapeDtypeStruct((batch * seq, ATTN_W), BF16),
        compiler_params=_params(),
        name="moba_prompt",
    )(q_bf, k_bf, v_bf, kmean)


PAGES_PER_STEP = 8
PAGES_PER_BLOCK = MOBA_BLOCK // PAGE_SIZE
PAGE_ROWS = PAGE_SIZE * N_HEADS


def _cache_select_kernel(pt_ref, q_ref, *refs):
    page_refs = refs[:PAGES_PER_STEP]
    sel_ref = refs[PAGES_PER_STEP]
    km_ref = refs[PAGES_PER_STEP + 1]
    g = pl.program_id(1)
    blocks_per_step = PAGES_PER_STEP // PAGES_PER_BLOCK
    for t in range(blocks_per_step):
        tot = jnp.zeros((N_HEADS, HEAD_DIM), F32)
        for r in range(PAGES_PER_BLOCK):
            page = page_refs[t * PAGES_PER_BLOCK + r][0]
            tot = tot + jnp.sum(page.reshape(PAGE_SIZE, N_HEADS, HEAD_DIM), axis=0)
        km_ref[g * blocks_per_step + t] = tot * (1.0 / MOBA_BLOCK)

    @pl.when(g == pl.num_programs(1) - 1)
    def _():
        nb = km_ref.shape[0]
        gate = jnp.sum(km_ref[...] * q_ref[0][None], axis=-1, keepdims=True)
        blk_f = lax.broadcasted_iota(jnp.int32, gate.shape, 0).astype(F32)
        for r in range(MOBA_TOP_K):
            best = jnp.max(gate, axis=0, keepdims=True)
            first = jnp.min(jnp.where(gate == best, blk_f, float(nb)), axis=0, keepdims=True)
            sel_ref[0, r] = jnp.broadcast_to(first[0], (N_HEADS, LANES)).astype(jnp.int32)
            gate = jnp.where(blk_f == first, NEG_INF, gate)


def _cache_select(page_table_flat, q_heads, cache_k_pages, n_samples, n_pages):
    steps = n_pages // PAGES_PER_STEP

    def page_spec(r):
        return pl.BlockSpec(
            (1, PAGE_ROWS, HEAD_DIM),
            lambda b, g, pt: (pt[b * n_pages + g * PAGES_PER_STEP + r], 0, 0))

    grid_spec = pltpu.PrefetchScalarGridSpec(
        num_scalar_prefetch=1,
        grid=(n_samples, steps),
        in_specs=[pl.BlockSpec((1, N_HEADS, HEAD_DIM), lambda b, g, pt: (b, 0, 0))]
                 + [page_spec(r) for r in range(PAGES_PER_STEP)],
        out_specs=pl.BlockSpec((1, MOBA_TOP_K, N_HEADS, LANES), lambda b, g, pt: (b, 0, 0, 0)),
        scratch_shapes=[pltpu.VMEM((n_pages // PAGES_PER_BLOCK, N_HEADS, HEAD_DIM), F32)],
    )
    return pl.pallas_call(
        _cache_select_kernel,
        grid_spec=grid_spec,
        out_shape=jax.ShapeDtypeStruct((n_samples, MOBA_TOP_K, N_HEADS, LANES), jnp.int32),
        compiler_params=_params(),
        name="cache_select",
    )(page_table_flat, q_heads, *([cache_k_pages] * PAGES_PER_STEP))


N_SEL_PAGES = MOBA_TOP_K * PAGES_PER_BLOCK


def _moba_step_kernel(pt_ref, sel_ref, q_ref, kn_ref, vn_ref, *refs):
    k_refs = refs[:N_SEL_PAGES]
    v_refs = refs[N_SEL_PAGES:2 * N_SEL_PAGES]
    o_ref = refs[2 * N_SEL_PAGES]
    head_rows = pl.ds(pl.program_id(1), PAGE_SIZE, stride=N_HEADS)
    scale = HEAD_DIM ** -0.5
    q = q_ref[0]
    s_own = jnp.sum(q * kn_ref[0], axis=1, keepdims=True) * scale
    s_pages = [jnp.sum(kr[0, head_rows, :] * q, axis=1, keepdims=True) * scale
               for kr in k_refs]
    m = s_own
    for s in s_pages:
        m = jnp.maximum(m, jnp.max(s, axis=0, keepdims=True))
    p_own = jnp.exp(s_own - m)
    l = p_own
    o = p_own * vn_ref[0]
    for s, vr in zip(s_pages, v_refs):
        p = jnp.exp(s - m)
        l = l + jnp.sum(p, axis=0, keepdims=True)
        o = o + jnp.sum(p * vr[0, head_rows, :], axis=0, keepdims=True)
    o_ref[0] = (o / l).astype(o_ref.dtype)


def _moba_step(page_table_flat, sel_flat, q_s, k_new, v_new, cache_k_pages, cache_v_pages,
               n_samples, n_pages):
    def page_spec(r, t):
        def index(b, h, pt, sel):
            blk = sel[(b * MOBA_TOP_K + r) * N_HEADS + h]
            return (pt[b * n_pages + blk * PAGES_PER_BLOCK + t], 0, 0)
        return pl.BlockSpec((1, PAGE_ROWS, HEAD_DIM), index)

    pages = [page_spec(r, t) for r in range(MOBA_TOP_K) for t in range(PAGES_PER_BLOCK)]
    head_spec = pl.BlockSpec((1, 1, HEAD_DIM), lambda b, h, pt, sel: (b, 0, h))
    grid_spec = pltpu.PrefetchScalarGridSpec(
        num_scalar_prefetch=2,
        grid=(n_samples, N_HEADS),
        in_specs=[head_spec, head_spec, head_spec] + pages + pages,
        out_specs=head_spec,
    )
    return pl.pallas_call(
        _moba_step_kernel,
        grid_spec=grid_spec,
        out_shape=jax.ShapeDtypeStruct((n_samples, 1, ATTN_W), BF16),
        compiler_params=_params(),
        name="moba_step",
    )(page_table_flat, sel_flat, q_s, k_new, v_new,
      *([cache_k_pages] * N_SEL_PAGES), *([cache_v_pages] * N_SEL_PAGES))


def _merge_kernel(act_ref, attn_ref, ga_ref, gb_ref, wc_ref, wa_ref, m_ref):
    a = _dot(act_ref[...], wc_ref[...])
    b = _dot(attn_ref[...], wa_ref[...])
    m_ref[...] = (ga_ref[...] * a + gb_ref[...] * b).astype(m_ref.dtype)


def _merge(act, attn, gates, w_conv_out, w_attn_out, tm):
    m = act.shape[0]
    return pl.pallas_call(
        _merge_kernel,
        grid=(m // tm,),
        in_specs=[pl.BlockSpec((tm, CONV_CH), lambda i: (i, 0)),
                  pl.BlockSpec((tm, ATTN_W), lambda i: (i, 0)),
                  pl.BlockSpec((tm, D_MODEL), lambda i: (i, 0)),
                  pl.BlockSpec((tm, D_MODEL), lambda i: (i, 1)),
                  pl.BlockSpec((CONV_CH, D_MODEL), lambda i: (0, 0)),
                  pl.BlockSpec((ATTN_W, D_MODEL), lambda i: (0, 0))],
        out_specs=pl.BlockSpec((tm, D_MODEL), lambda i: (i, 0)),
        out_shape=jax.ShapeDtypeStruct((m, D_MODEL), BF16),
        compiler_params=_params(),
        name="merge",
    )(act, attn, gates, gates, w_conv_out, w_attn_out)


def _out_proj_kernel(m_ref, x_ref, w_ref, g_ref, h_ref, hn_ref):
    h = x_ref[...] + _dot(m_ref[...], w_ref[...])
    h_ref[...] = h
    ms = jnp.mean(h * h, axis=-1, keepdims=True)
    hn_ref[...] = (h * lax.rsqrt(ms + EPS) * g_ref[...]).astype(hn_ref.dtype)


def _out_proj(mixed, x, w_out, norm2_g, tm):
    m = x.shape[0]
    row_spec = pl.BlockSpec((tm, D_MODEL), lambda i: (i, 0))
    return pl.pallas_call(
        _out_proj_kernel,
        grid=(m // tm,),
        in_specs=[row_spec, row_spec,
                  pl.BlockSpec((D_MODEL, D_MODEL), lambda i: (0, 0)),
                  pl.BlockSpec((1, D_MODEL), lambda i: (0, 0))],
        out_specs=[row_spec, row_spec],
        out_shape=[jax.ShapeDtypeStruct((m, D_MODEL), F32),
                   jax.ShapeDtypeStruct((m, D_MODEL), BF16)],
        compiler_params=_params(),
        name="out_proj",
    )(mixed, x, w_out, norm2_g.reshape(1, D_MODEL))


FFN_HALO = 16


def _ffn_conv3(s_ref, up, halo, w_ref, b_ref):
    tm = up.shape[0]
    s_ref[0:FFN_HALO, :] = halo
    s_ref[FFN_HALO:, :] = up
    return (s_ref[FFN_HALO - 2:FFN_HALO - 2 + tm, :] * w_ref[0:1, :]
            + s_ref[FFN_HALO - 1:FFN_HALO - 1 + tm, :] * w_ref[1:2, :]
            + up * w_ref[2:3, :] + b_ref[...])


def _ffn_prompt_kernel(tiles_per_seq, hn_ref, halo_ref, wg_ref, wv_ref, cwg_ref, cwv_ref,
                       cbg_ref, cbv_ref, wd_ref, h_ref, y_ref, acc_ref, sg_ref, sv_ref):
    i = pl.program_id(0)
    f = pl.program_id(1)
    hn = hn_ref[...]
    seq_start = (i % tiles_per_seq) == 0
    hn_halo = jnp.where(seq_start, jnp.zeros_like(halo_ref[...]), halo_ref[...])
    wg = wg_ref[...]
    wv = wv_ref[...]
    cg = _ffn_conv3(sg_ref, _dot(hn, wg), _dot(hn_halo, wg), cwg_ref, cbg_ref)
    cv = _ffn_conv3(sv_ref, _dot(hn, wv), _dot(hn_halo, wv), cwv_ref, cbv_ref)
    act = (cg * _sigmoid(cg) * cv).astype(BF16)
    part = _dot(act, wd_ref[...])

    @pl.when(f == 0)
    def _():
        acc_ref[...] = part

    @pl.when(f > 0)
    def _():
        acc_ref[...] += part

    @pl.when(f == pl.num_programs(1) - 1)
    def _():
        y_ref[...] = h_ref[...] + acc_ref[...]


def _ffn_prompt(hn, h, w_up, cw, cb, w_down, seq, tm):
    m = hn.shape[0]
    nf = N_FFN_TILES
    halo_per_tile = tm // FFN_HALO
    row_spec = pl.BlockSpec((tm, D_MODEL), lambda i, f: (i, 0))
    return pl.pallas_call(
        functools.partial(_ffn_prompt_kernel, seq // tm),
        grid=(m // tm, nf),
        in_specs=[row_spec,
                  pl.BlockSpec((FFN_HALO, D_MODEL), lambda i, f: (jnp.maximum(i * halo_per_tile - 1, 0), 0)),
                  pl.BlockSpec((D_MODEL, FFN_TILE), lambda i, f: (0, f)),
                  pl.BlockSpec((D_MODEL, FFN_TILE), lambda i, f: (0, nf + f)),
                  pl.BlockSpec((FFN_CONV_WIDTH, FFN_TILE), lambda i, f: (0, f)),
                  pl.BlockSpec((FFN_CONV_WIDTH, FFN_TILE), lambda i, f: (0, nf + f)),
                  pl.BlockSpec((1, FFN_TILE), lambda i, f: (0, f)),
                  pl.BlockSpec((1, FFN_TILE), lambda i, f: (0, nf + f)),
                  pl.BlockSpec((FFN_TILE, D_MODEL), lambda i, f: (f, 0)),
                  row_spec],
        out_specs=row_spec,
        out_shape=jax.ShapeDtypeStruct((m, D_MODEL), F32),
        scratch_shapes=[pltpu.VMEM((tm, D_MODEL), F32),
                        pltpu.VMEM((tm + FFN_HALO, FFN_TILE), F32),
                        pltpu.VMEM((tm + FFN_HALO, FFN_TILE), F32)],
        compiler_params=_params(),
        name="ffn_prompt",
    )(hn, hn, w_up, w_up, cw, cw, cb, cb, w_down, h)


def _ffn_step_kernel(hn_ref, wg_ref, wv_ref, s0g_ref, s1g_ref, s0v_ref, s1v_ref, cwg_ref, cwv_ref,
                     cbg_ref, cbv_ref, wd_ref, h_ref, y_ref, upg_ref, upv_ref, acc_ref):
    f = pl.program_id(0)
    hn = hn_ref[...]
    upg = _dot(hn, wg_ref[...])
    upv = _dot(hn, wv_ref[...])
    upg_ref[...] = upg
    upv_ref[...] = upv
    cg = s0g_ref[...] * cwg_ref[0:1, :] + s1g_ref[...] * cwg_ref[1:2, :] + upg * cwg_ref[2:3, :] + cbg_ref[...]
    cv = s0v_ref[...] * cwv_ref[0:1, :] + s1v_ref[...] * cwv_ref[1:2, :] + upv * cwv_ref[2:3, :] + cbv_ref[...]
    act = (cg * _sigmoid(cg) * cv).astype(BF16)
    part = _dot(act, wd_ref[...])

    @pl.when(f == 0)
    def _():
        acc_ref[...] = part

    @pl.when(f > 0)
    def _():
        acc_ref[...] += part

    @pl.when(f == pl.num_programs(0) - 1)
    def _():
        y_ref[...] = h_ref[...] + acc_ref[...]


def _ffn_step(hn, h, s0, s1, w_up, cw, cb, w_down):
    m = hn.shape[0]
    nf = N_FFN_TILES
    full = pl.BlockSpec((m, D_MODEL), lambda f: (0, 0))
    lo = lambda rows: pl.BlockSpec((rows, FFN_TILE), lambda f: (0, f))
    hi = lambda rows: pl.BlockSpec((rows, FFN_TILE), lambda f: (0, nf + f))
    return pl.pallas_call(
        _ffn_step_kernel,
        grid=(nf,),
        in_specs=[full,
                  pl.BlockSpec((D_MODEL, FFN_TILE), lambda f: (0, f)),
                  pl.BlockSpec((D_MODEL, FFN_TILE), lambda f: (0, nf + f)),
                  lo(m), lo(m), hi(m), hi(m),
                  lo(FFN_CONV_WIDTH), hi(FFN_CONV_WIDTH), lo(1), hi(1),
                  pl.BlockSpec((FFN_TILE, D_MODEL), lambda f: (f, 0)),
                  full],
        out_specs=[full, lo(m), lo(m)],
        out_shape=[jax.ShapeDtypeStruct((m, D_MODEL), F32),
                   jax.ShapeDtypeStruct((m, D_FF_PAD), F32),
                   jax.ShapeDtypeStruct((m, D_FF_PAD), F32)],
        scratch_shapes=[pltpu.VMEM((m, D_MODEL), F32)],
        compiler_params=_params(),
        name="ffn_step",
    )(hn, w_up, w_up, s0, s1, s0, s1, cw, cw, cb, cb, w_down, h)


def _pad_ffn_cols(a):
    pad = [(0, 0)] * (a.ndim - 1) + [(0, D_FF_PAD - D_FF)]
    return jnp.concatenate([jnp.pad(a[..., :D_FF], pad), jnp.pad(a[..., D_FF:], pad)], axis=-1)


def kernel(x_prompt, x_sample, cache_k, cache_v, page_table, state_conv, state_ffn_conv, norm1_g, w_in, b_gate, q_norm_g, k_norm_g, w_dw, b_dw, ln_g, ln_b, w_conv_out, w_attn_out, w_out, norm2_g, w_ffn_up, w_ffn_dw, b_ffn_dw, w_ffn_down):
    assert w_in.shape[0] == 1, "single-layer trunk"
    batch, seq, _ = x_prompt.shape
    n_samples = x_sample.shape[0]
    n_pages = page_table.shape[1]
    n_pool = cache_k.shape[1]
    mp = batch * seq

    w_in_bf = w_in[0].astype(BF16)
    w_conv_out_bf = w_conv_out[0].astype(BF16)
    w_attn_out_bf = w_attn_out[0].astype(BF16)
    w_out_bf = w_out[0].astype(BF16)
    w_up_bf = _pad_ffn_cols(w_ffn_up[0]).astype(BF16)
    w_down_bf = jnp.pad(w_ffn_down[0], ((0, D_FF_PAD - D_FF), (0, 0))).astype(BF16)
    cw = _pad_ffn_cols(w_ffn_dw[0])
    cb = _pad_ffn_cols(b_ffn_dw[0]).reshape(1, 2 * D_FF_PAD)

    xp = x_prompt.reshape(mp, D_MODEL)
    xn = _rmsnorm(xp, norm1_g[0], 1024)
    u = _glu_proj(xn, w_in_bf, 1024)
    q_bf, (k, k_bf, kmean), v, v_bf = _qkv_proj(xn, w_in_bf, q_norm_g[0], k_norm_g[0], 1024, BF16, True)
    gates = _gate_proj(xn, w_in_bf, b_gate[0], 1024)
    act = _conv_act(u.reshape(batch, seq, CONV_CH), w_dw[0], b_dw[0], ln_g[0], ln_b[0], 512)
    attn = _moba_prompt(q_bf, k_bf, v_bf, kmean.reshape(mp // MOBA_BLOCK, ATTN_W), batch, seq)
    mixed = _merge(act.reshape(mp, CONV_CH), attn, gates, w_conv_out_bf, w_attn_out_bf, 512)
    h, hn = _out_proj(mixed, xp, w_out_bf, norm2_g[0], 512)
    y = _ffn_prompt(hn, h, w_up_bf, cw, cb, w_down_bf, seq, 512)

    xs = x_sample.reshape(n_samples, D_MODEL)
    xsn = _rmsnorm(xs, norm1_g[0], n_samples)
    u_s = _glu_proj(xsn, w_in_bf, n_samples)
    q_s, (k_s, _), v_s, _ = _qkv_proj(xsn, w_in_bf, q_norm_g[0], k_norm_g[0], n_samples, F32, False)
    gates_s = _gate_proj(xsn, w_in_bf, b_gate[0], n_samples)
    act_s = _conv_step(state_conv[0], u_s, w_dw[0], b_dw[0], ln_g[0], ln_b[0])

    pt_flat = page_table.reshape(-1)
    k_pages = cache_k.reshape(n_pool, PAGE_ROWS, HEAD_DIM)
    v_pages = cache_v.reshape(n_pool, PAGE_ROWS, HEAD_DIM)
    q_s3 = q_s.reshape(n_samples, 1, ATTN_W)
    sel = _cache_select(pt_flat, q_s.reshape(n_samples, N_HEADS, HEAD_DIM), k_pages, n_samples, n_pages)
    sel_flat = sel[:, :, :, 0].reshape(-1)
    attn_s = _moba_step(pt_flat, sel_flat, q_s3, k_s.reshape(n_samples, 1, ATTN_W),
                        v_s.reshape(n_samples, 1, ATTN_W), k_pages, v_pages, n_samples, n_pages)
    mixed_s = _merge(act_s, attn_s.reshape(n_samples, ATTN_W), gates_s, w_conv_out_bf, w_attn_out_bf,
                     n_samples)
    h_s, hn_s = _out_proj(mixed_s, xs, w_out_bf, norm2_g[0], n_samples)

    tail_rows = jnp.concatenate([hn.reshape(batch, seq, D_MODEL)[:, seq - 2:, :].reshape(2 * batch, D_MODEL),
                                 jnp.zeros((16 - 2 * batch - n_samples % 16, D_MODEL), BF16)], axis=0)
    rows = n_samples + tail_rows.shape[0]
    hn_rows = jnp.concatenate([hn_s, tail_rows], axis=0)
    h_rows = jnp.concatenate([h_s, jnp.zeros((rows - n_samples, D_MODEL), F32)], axis=0)
    state_rows = jnp.pad(_pad_ffn_cols(state_ffn_conv[0]), ((0, rows - n_samples), (0, 0), (0, 0)))
    y_rows, upg, upv = _ffn_step(hn_rows, h_rows, state_rows[:, 0], state_rows[:, 1],
                                 w_up_bf, cw, cb, w_down_bf)
    up_rows = jnp.concatenate([upg[:, :D_FF], upv[:, :D_FF]], axis=-1)

    kv_p = lambda a: a.reshape(1, batch, seq, N_HEADS, HEAD_DIM)
    kv_s = lambda a: a.reshape(1, n_samples, 1, N_HEADS, HEAD_DIM)
    conv_state_prompt = u.reshape(batch, seq, CONV_CH)[:, seq - (CONV_WIDTH - 1):][None]
    conv_state_sample = jnp.concatenate([state_conv[0][:, 1:], u_s[:, None, :]], axis=1)[None]
    ffn_state_prompt = up_rows[n_samples:n_samples + 2 * batch].reshape(1, batch, 2, 2 * D_FF)
    ffn_state_sample = jnp.concatenate([state_ffn_conv[0][:, 1:], up_rows[:n_samples, None, :]], axis=1)[None]
    return (y.reshape(batch, seq, D_MODEL), y_rows[:n_samples].reshape(n_samples, 1, D_MODEL),
            kv_p(k), kv_p(v), kv_s(k_s), kv_s(v_s),
            conv_state_prompt, conv_state_sample, ffn_state_prompt, ffn_state_sample)
```
